```python
import math
import jax, jax.numpy as jnp
from jax import lax
import numpy as np

D_MODEL = 4096
BATCH = 4
SEQ = 2048
DEPTH = 1
DEC_BATCH = 128
DEC_SEQ = 8
PAST_LEN = 2048
PAGE_SIZE = 128

POOL_WIDTH = D_MODEL // 4
POOL_WINDOWS = (2, 4, 8, 16)
POOL_GROUP = POOL_WIDTH // len(POOL_WINDOWS)
POOL_BUF = max(POOL_WINDOWS) - 1
ATTN_WIDTH = D_MODEL - POOL_WIDTH
HEAD_DIM = 128
QK_HEAD = 2 * HEAD_DIM
V_HEAD = 2 * HEAD_DIM
N_HEADS = ATTN_WIDTH // V_HEAD
IN_WIDTH = POOL_WIDTH + 2 * N_HEADS * QK_HEAD + N_HEADS * V_HEAD
Q_BLOCK = 128
NUM_BUCKETS = 32
MAX_DISTANCE = 128
N_GROUPS = 4
EXPERTS_PER_GROUP = 4
N_EXPERTS = N_GROUPS * EXPERTS_PER_GROUP
TOP_K_IN_GROUP = 2
EXPERT_FF = D_MODEL // 4
PLE_DIM = 256
ALPHA = (2.0 * DEPTH) ** 0.25
BETA = (8.0 * DEPTH) ** -0.25
LN_EPS = 1e-5
RMS_EPS = 1e-5
NEG = -1e30

kernel_name = 'hymba_pool_diffattn_hmoe_deepnorm_step'


def layer_norm(x, g, b):
    xf = x.astype(jnp.float32)
    mu = jnp.mean(xf, -1, keepdims=True)
    var = jnp.mean(jnp.square(xf - mu), -1, keepdims=True)
    return ((xf - mu) * lax.rsqrt(var + LN_EPS) * g + b).astype(x.dtype)


def rel_bias_map(q_pos, k_pos, table):
    n = jnp.maximum(q_pos[:, None] - k_pos[None, :], 0)
    max_exact = NUM_BUCKETS // 2
    nf = jnp.maximum(n, 1).astype(jnp.float32)
    large = max_exact + (jnp.log(nf / max_exact) / math.log(MAX_DISTANCE / max_exact)
                         * (NUM_BUCKETS - max_exact)).astype(jnp.int32)
    large = jnp.minimum(large, NUM_BUCKETS - 1)
    bucket = jnp.where(n < max_exact, n, large)
    return jnp.moveaxis(table[bucket].astype(jnp.float32), -1, 0)


def diff_attend(q, k, v, q_pos, k_pos, lam, table):
    scale = HEAD_DIM ** -0.5
    bias = rel_bias_map(q_pos, k_pos, table)
    causal = k_pos[None, :] <= q_pos[:, None]

    def probs(qh, kh):
        s = jnp.einsum('...qhd,...khd->...hqk', qh, kh).astype(jnp.float32) * scale + bias
        return jax.nn.softmax(jnp.where(causal, s, NEG), axis=-1)

    a = probs(q[..., :HEAD_DIM], k[..., :HEAD_DIM]) - lam * probs(q[..., HEAD_DIM:], k[..., HEAD_DIM:])
    return jnp.einsum('...hqk,...khd->...qhd', a.astype(v.dtype), v)


def prompt_attention(q, k, v, lam, table):
    n, L = q.shape[0], q.shape[1]
    k_pos = jnp.arange(L)

    def block(i):
        start = i * Q_BLOCK
        qb = lax.dynamic_slice_in_dim(q, start, Q_BLOCK, axis=1)
        return diff_attend(qb, k, v, start + jnp.arange(Q_BLOCK), k_pos, lam, table)

    o = lax.map(block, jnp.arange(L // Q_BLOCK))
    return jnp.moveaxis(o, 0, 1).reshape(n, L, N_HEADS, V_HEAD)


def sample_attention(q, k, v, lam, table, cache_k, cache_v, layer, page_table):
    Ls = q.shape[1]
    q_pos = PAST_LEN + jnp.arange(Ls)
    k_pos = jnp.arange(PAST_LEN + Ls)

    def one(args):
        qs, ks, vs, pages = args
        kc = cache_k[layer, pages].reshape(-1, N_HEADS, QK_HEAD)
        vc = cache_v[layer, pages].reshape(-1, N_HEADS, V_HEAD)
        return diff_attend(qs, jnp.concatenate([kc, ks], 0), jnp.concatenate([vc, vs], 0),
                           q_pos, k_pos, lam, table)

    return lax.map(one, (q, k, v, page_table))


def pool_mixer(u, buf, start_pos, w_mix, scale):
    L = u.shape[1]
    up = jnp.concatenate([buf, u], axis=1)
    upf = up.astype(jnp.float32)
    cs = jnp.concatenate([jnp.zeros_like(upf[:, :1]), jnp.cumsum(upf, axis=1)], axis=1)
    pos = start_pos + jnp.arange(L)
    outs = []
    for g, w in enumerate(POOL_WINDOWS):
        c0, c1 = g * POOL_GROUP, (g + 1) * POOL_GROUP
        s = cs[:, POOL_BUF + 1:, c0:c1] - cs[:, POOL_BUF + 1 - w:POOL_BUF + 1 - w + L, c0:c1]
        cnt = jnp.minimum(pos + 1, w).astype(jnp.float32)[None, :, None]
        d = s / cnt - upf[:, POOL_BUF:, c0:c1]
        outs.append(jnp.einsum('nlc,cd->nld', d.astype(u.dtype), w_mix[g]))
    y = jnp.concatenate(outs, -1) * scale
    return y, up[:, -POOL_BUF:]


def token_mixer(h, pool_buf, start_pos, attn_fn, lam, lam_init, lp):
    N, L, _ = h.shape
    z = jnp.einsum('nld,de->nle', h, lp['w_in'])
    o1 = POOL_WIDTH
    o2 = o1 + N_HEADS * QK_HEAD
    o3 = o2 + N_HEADS * QK_HEAD
    u = z[..., :o1]
    q = z[..., o1:o2].reshape(N, L, N_HEADS, QK_HEAD)
    k = z[..., o2:o3].reshape(N, L, N_HEADS, QK_HEAD)
    v = z[..., o3:].reshape(N, L, N_HEADS, V_HEAD)
    pool_out, new_buf = pool_mixer(u, pool_buf, start_pos, lp['w_pool_mix'], lp['pool_scale'])
    of = attn_fn(q, k, v, lam).astype(jnp.float32)
    of = of * lax.rsqrt(jnp.mean(jnp.square(of), -1, keepdims=True) + RMS_EPS) * lp['head_gain'] * (1.0 - lam_init)
    cat = jnp.concatenate([pool_out, of.astype(h.dtype).reshape(N, L, ATTN_WIDTH)], -1)
    return jnp.einsum('nle,ed->nld', cat, lp['w_out']), new_buf, k, v


def hier_moe(h, lp):
    N, L, D = h.shape
    t = h.reshape(-1, D)
    g_logits = jnp.einsum('td,dg->tg', t, lp['router_group_w']).astype(jnp.float32) + lp['router_group_b']
    g_prob = jax.nn.softmax(g_logits, -1)
    g_idx = jnp.argmax(g_logits, -1)
    g_w = jnp.take_along_axis(g_prob, g_idx[:, None], -1)
    e_logits = jnp.einsum('td,gde->tge', t, lp['router_expert_w']).astype(jnp.float32) + lp['router_expert_b']
    e_sel = jnp.take_along_axis(e_logits, g_idx[:, None, None], axis=1)[:, 0]
    top_v, top_i = lax.top_k(e_sel, TOP_K_IN_GROUP)
    top_w = jax.nn.softmax(top_v, -1) * g_w
    eid = g_idx[:, None] * EXPERTS_PER_GROUP + top_i
    comb = jnp.sum(jax.nn.one_hot(eid, N_EXPERTS, dtype=jnp.float32) * top_w[..., None], axis=1)
    hg = jnp.einsum('td,edf->tef', t, lp['w_gate'])
    hu = jnp.einsum('td,edf->tef', t, lp['w_up'])
    a = jax.nn.silu(hg) * hu * comb[..., None].astype(t.dtype)
    y = jnp.einsum('tef,efd->td', a, lp['w_down'])
    return y.reshape(N, L, D)


def apply_layer(h, p, pool_buf, start_pos, attn_fn, lam, lam_init, lp):
    mix, new_buf, k, v = token_mixer(h, pool_buf, start_pos, attn_fn, lam, lam_init, lp)
    h = layer_norm(ALPHA * h + mix, lp['ln1_g'], lp['ln1_b'])
    h = layer_norm(ALPHA * h + hier_moe(h, lp), lp['ln2_g'], lp['ln2_b'])
    gate = jax.nn.sigmoid(jnp.einsum('nld,de->nle', h, lp['w_ple_gate']) + lp['b_ple_gate'])
    h = h + gate * jnp.einsum('nlp,pd->nld', p, lp['w_ple_proj'])
    return h, k, v, new_buf


def setup_inputs(seed: int = 0) -> dict:
    key = jax.random.key(seed)
    ks = jax.random.split(key, 40)
    f32 = jnp.float32
    nrm = lambda k, s: jax.random.normal(k, s, f32)
    n_pages = PAST_LEN // PAGE_SIZE
    n_used = DEC_BATCH * n_pages
    n_pool = n_used + max(1, n_used // 4)
    page_table = jax.random.permutation(ks[7], n_pool)[:n_used].reshape(DEC_BATCH, n_pages).astype(jnp.int32)
    col_scale = jnp.concatenate([jnp.ones((IN_WIDTH - N_HEADS * V_HEAD,), f32),
                                 jnp.full((N_HEADS * V_HEAD,), BETA, f32)])
    return {
        'x_prompt': nrm(ks[0], (BATCH, SEQ, D_MODEL)),
        'x_sample': nrm(ks[1], (DEC_BATCH, DEC_SEQ, D_MODEL)),
        'p_prompt': nrm(ks[2], (DEPTH, BATCH, SEQ, PLE_DIM)),
        'p_sample': nrm(ks[3], (DEPTH, DEC_BATCH, DEC_SEQ, PLE_DIM)),
        'cache_k': nrm(ks[4], (DEPTH, n_pool, PAGE_SIZE, N_HEADS, QK_HEAD)),
        'cache_v': nrm(ks[5], (DEPTH, n_pool, PAGE_SIZE, N_HEADS, V_HEAD)),
        'state_pool': nrm(ks[6], (DEPTH, DEC_BATCH, POOL_BUF, POOL_WIDTH)),
        'page_table': page_table,
        'ln_in_g': 1.0 + 0.02 * nrm(ks[8], (D_MODEL,)),
        'ln_in_b': 0.02 * nrm(ks[9], (D_MODEL,)),
        'w_in': nrm(ks[10], (DEPTH, D_MODEL, IN_WIDTH)) * (D_MODEL ** -0.5) * col_scale,
        'w_pool_mix': nrm(ks[11], (DEPTH, len(POOL_WINDOWS), POOL_GROUP, POOL_GROUP)) * (POOL_GROUP ** -0.5),
        'pool_scale': 1.0 + 0.02 * nrm(ks[12], (DEPTH, POOL_WIDTH)),
        'lambda_q1': 0.1 * nrm(ks[13], (DEPTH, HEAD_DIM)),
        'lambda_k1': 0.1 * nrm(ks[14], (DEPTH, HEAD_DIM)),
        'lambda_q2': 0.1 * nrm(ks[15], (DEPTH, HEAD_DIM)),
        'lambda_k2': 0.1 * nrm(ks[16], (DEPTH, HEAD_DIM)),
        'head_gain': 1.0 + 0.02 * nrm(ks[17], (DEPTH, V_HEAD)),
        'rel_bias': 0.5 * nrm(ks[18], (NUM_BUCKETS, N_HEADS)),
        'w_out': nrm(ks[19], (DEPTH, D_MODEL, D_MODEL)) * (D_MODEL ** -0.5) * BETA,
        'ln1_g': 1.0 + 0.02 * nrm(ks[20], (DEPTH, D_MODEL)),
        'ln1_b': 0.02 * nrm(ks[21], (DEPTH, D_MODEL)),
        'router_group_w': nrm(ks[22], (DEPTH, D_MODEL, N_GROUPS)) * (D_MODEL ** -0.5),
        'router_group_b': 0.01 * nrm(ks[23], (DEPTH, N_GROUPS)),
        'router_expert_w': nrm(ks[24], (DEPTH, N_GROUPS, D_MODEL, EXPERTS_PER_GROUP)) * (D_MODEL ** -0.5),
        'router_expert_b': 0.01 * nrm(ks[25], (DEPTH, N_GROUPS, EXPERTS_PER_GROUP)),
        'w_gate': nrm(ks[26], (DEPTH, N_EXPERTS, D_MODEL, EXPERT_FF)) * (D_MODEL ** -0.5),
        'w_up': nrm(ks[27], (DEPTH, N_EXPERTS, D_MODEL, EXPERT_FF)) * (D_MODEL ** -0.5),
        'w_down': nrm(ks[28], (DEPTH, N_EXPERTS, EXPERT_FF, D_MODEL)) * (EXPERT_FF ** -0.5) * BETA,
        'ln2_g': 1.0 + 0.02 * nrm(ks[29], (DEPTH, D_MODEL)),
        'ln2_b': 0.02 * nrm(ks[30], (DEPTH, D_MODEL)),
        'w_ple_proj': nrm(ks[31], (DEPTH, PLE_DIM, D_MODEL)) * (PLE_DIM ** -0.5),
        'w_ple_gate': nrm(ks[32], (DEPTH, D_MODEL, D_MODEL)) * (D_MODEL ** -0.5),
        'b_ple_gate': 0.02 * nrm(ks[33], (DEPTH, D_MODEL)),
    }


def reference(x_prompt, x_sample, p_prompt, p_sample, cache_k, cache_v, state_pool, page_table,
              ln_in_g, ln_in_b, w_in, w_pool_mix, pool_scale, lambda_q1, lambda_k1, lambda_q2, lambda_k2,
              head_gain, rel_bias, w_out, ln1_g, ln1_b, router_group_w, router_group_b,
              router_expert_w, router_expert_b, w_gate, w_up, w_down, ln2_g, ln2_b,
              w_ple_proj, w_ple_gate, b_ple_gate):
    hp = layer_norm(x_prompt, ln_in_g, ln_in_b)
    hs = layer_norm(x_sample, ln_in_g, ln_in_b)
    kp_l, vp_l, ks_l, vs_l, bp_l, bs_l = [], [], [], [], [], []
    for l in range(DEPTH):
        lam_init = 0.8 - 0.6 * math.exp(-0.3 * l)
        lam = (jnp.exp(jnp.sum(lambda_q1[l] * lambda_k1[l]).astype(jnp.float32))
               - jnp.exp(jnp.sum(lambda_q2[l] * lambda_k2[l]).astype(jnp.float32)) + lam_init)
        lp = dict(w_in=w_in[l], w_pool_mix=w_pool_mix[l], pool_scale=pool_scale[l], head_gain=head_gain[l],
                  w_out=w_out[l], ln1_g=ln1_g[l], ln1_b=ln1_b[l], router_group_w=router_group_w[l],
                  router_group_b=router_group_b[l], router_expert_w=router_expert_w[l],
                  router_expert_b=router_expert_b[l], w_gate=w_gate[l], w_up=w_up[l], w_down=w_down[l],
                  ln2_g=ln2_g[l], ln2_b=ln2_b[l], w_ple_proj=w_ple_proj[l], w_ple_gate=w_ple_gate[l],
                  b_ple_gate=b_ple_gate[l])
        attn_p = lambda q, k, v, lm: prompt_attention(q, k, v, lm, rel_bias)
        attn_s = lambda q, k, v, lm, _l=l: sample_attention(q, k, v, lm, rel_bias, cache_k, cache_v, _l, page_table)
        zero_buf = jnp.zeros((hp.shape[0], POOL_BUF, POOL_WIDTH), hp.dtype)
        hp, kp, vp, bp = apply_layer(hp, p_prompt[l], zero_buf, 0, attn_p, lam, lam_init, lp)
        hs, kss, vss, bs = apply_layer(hs, p_sample[l], state_pool[l], PAST_LEN, attn_s, lam, lam_init, lp)
        kp_l.append(kp); vp_l.append(vp); ks_l.append(kss); vs_l.append(vss); bp_l.append(bp); bs_l.append(bs)
    k_prompt = jnp.stack(kp_l)
    v_prompt = jnp.stack(vp_l)
    k_sample = jnp.stack(ks_l)
    v_sample = jnp.stack(vs_l)
    pool_prompt = jnp.stack(bp_l)
    pool_sample = jnp.stack(bs_l)
    return (hp, hs, k_prompt, v_prompt, k_sample, v_sample, pool_prompt, pool_sample)
```

```python
import functools
import math

import jax
import jax.numpy as jnp
from jax import lax
from jax.experimental import pallas as pl
from jax.experimental.pallas import tpu as pltpu

F32 = jnp.float32
MXU_DTYPE = jnp.bfloat16

V7X_VMEM_BYTES = 64 * 1024 * 1024
VMEM_LIMIT_BYTES = V7X_VMEM_BYTES - 8 * 1024 * 1024
LANES = 128
SUBLANES = 8

POOL_WINDOWS = (2, 4, 8, 16)
POOL_HALO = 16
HEAD_DIM = 128
QK_HEAD = 2 * HEAD_DIM
V_HEAD = 2 * HEAD_DIM
NUM_BUCKETS = 32
MAX_DISTANCE = 128
N_GROUPS = 4
EXPERTS_PER_GROUP = 4
LN_EPS = 1e-5
RMS_EPS = 1e-5
NEG = -1e30


def _cp(*sem):
    return pltpu.CompilerParams(dimension_semantics=sem, vmem_limit_bytes=VMEM_LIMIT_BYTES)


def _tile(n, pref, mult=SUBLANES):
    if n <= pref:
        return n
    t = (pref // mult) * mult
    while t > mult and n % t:
        t -= mult
    assert n % t == 0, (n, pref, mult)
    return t


def _ln_rows(x, g, b):
    mu = jnp.mean(x, -1, keepdims=True)
    xc = x - mu
    var = jnp.mean(xc * xc, -1, keepdims=True)
    return xc * lax.rsqrt(var + LN_EPS) * g + b


def _ln_in_kernel(xp_ref, xs_ref, g_ref, b_ref, h_ref, hb_ref, *, n_p):
    i = pl.program_id(0)

    def run(x_ref):
        y = _ln_rows(x_ref[...], g_ref[...], b_ref[...])
        h_ref[...] = y
        hb_ref[...] = y.astype(hb_ref.dtype)

    @pl.when(i < n_p)
    def _():
        run(xp_ref)

    @pl.when(i >= n_p)
    def _():
        run(xs_ref)


def _ln_in(xp, xs, g, b, tm):
    tp, d = xp.shape
    ts = xs.shape[0]
    n_p, n_s = tp // tm, ts // tm
    return pl.pallas_call(
        functools.partial(_ln_in_kernel, n_p=n_p),
        out_shape=(jax.ShapeDtypeStruct((tp + ts, d), F32), jax.ShapeDtypeStruct((tp + ts, d), MXU_DTYPE)),
        grid=(n_p + n_s,),
        in_specs=[
            pl.BlockSpec((tm, d), lambda i: (jnp.minimum(i, n_p - 1), 0)),
            pl.BlockSpec((tm, d), lambda i: (jnp.maximum(i - n_p, 0), 0)),
            pl.BlockSpec((1, d), lambda i: (0, 0)),
            pl.BlockSpec((1, d), lambda i: (0, 0)),
        ],
        out_specs=(pl.BlockSpec((tm, d), lambda i: (i, 0)), pl.BlockSpec((tm, d), lambda i: (i, 0))),
        compiler_params=_cp("arbitrary"),
        name="ln_in",
    )(xp, xs, g, b)


def _mm_kernel(a_ref, w_ref, o_ref, acc_ref, *, nk):
    k = pl.program_id(2)

    @pl.when(k == 0)
    def _():
        acc_ref[...] = jnp.zeros_like(acc_ref)

    acc_ref[...] += jnp.dot(a_ref[...], w_ref[...].astype(MXU_DTYPE), preferred_element_type=F32)

    @pl.when(k == nk - 1)
    def _():
        o_ref[...] = acc_ref[...].astype(o_ref.dtype)


def _mm(a, w, *, row0, rows, col0, cols, tm, tn, tk, out_dtype=F32, name="mm"):
    kdim = a.shape[1]
    tm = _tile(math.gcd(rows, row0), tm)
    tn = _tile(math.gcd(cols, col0), tn, LANES)
    tk = _tile(kdim, tk, LANES)
    rb, cb, nk = row0 // tm, col0 // tn, kdim // tk
    return pl.pallas_call(
        functools.partial(_mm_kernel, nk=nk),
        out_shape=jax.ShapeDtypeStruct((rows, cols), out_dtype),
        grid=(rows // tm, cols // tn, nk),
        in_specs=[
            pl.BlockSpec((tm, tk), lambda i, j, k: (i + rb, k)),
            pl.BlockSpec((tk, tn), lambda i, j, k: (k, j + cb)),
        ],
        out_specs=pl.BlockSpec((tm, tn), lambda i, j, k: (i, j)),
        scratch_shapes=[pltpu.VMEM((tm, tn), F32)],
        compiler_params=_cp("parallel", "parallel", "arbitrary"),
        name=name,
    )(a, w)


def _mm_heads_kernel(a_ref, w_ref, o_ref, acc_ref, *, nk, head_w):
    k = pl.program_id(2)

    @pl.when(k == 0)
    def _():
        acc_ref[...] = jnp.zeros_like(acc_ref)

    acc_ref[...] += jnp.dot(a_ref[...], w_ref[...].astype(MXU_DTYPE), preferred_element_type=F32)

    @pl.when(k == nk - 1)
    def _():
        bsub, hpb, lsub, _ = o_ref.shape
        for hh in range(hpb):
            o_ref[:, hh] = acc_ref[:, hh * head_w:(hh + 1) * head_w].reshape(bsub, lsub, head_w)


def _mm_heads(a, w, *, row0, nb, l, col0, n_heads, head_w, tm, tn, tk, name):
    kdim = a.shape[1]
    rows, cols = nb * l, n_heads * head_w
    if l % SUBLANES == 0 and l <= tm:
        lsub = l
        bsub = _tile(math.gcd(nb, row0 // l), max(tm // l, 1), 1)
    else:
        lsub, bsub = _tile(math.gcd(l, row0), tm), 1
    tm = bsub * lsub
    tn = _tile(math.gcd(cols, col0), tn, head_w)
    tk = _tile(kdim, tk, LANES)
    hpb = tn // head_w
    rb, cb, nk = row0 // tm, col0 // tn, kdim // tk
    lblocks = l // lsub
    return pl.pallas_call(
        functools.partial(_mm_heads_kernel, nk=nk, head_w=head_w),
        out_shape=jax.ShapeDtypeStruct((nb, n_heads, l, head_w), F32),
        grid=(rows // tm, cols // tn, nk),
        in_specs=[
            pl.BlockSpec((tm, tk), lambda i, j, k: (i + rb, k)),
            pl.BlockSpec((tk, tn), lambda i, j, k: (k, j + cb)),
        ],
        out_specs=pl.BlockSpec((bsub, hpb, lsub, head_w), lambda i, j, k: (i // lblocks, j, i % lblocks, 0)),
        scratch_shapes=[pltpu.VMEM((tm, tn), F32)],
        compiler_params=_cp("parallel", "parallel", "arbitrary"),
        name=name,
    )(a, w)


def _pool_kernel(u_ref, halo_ref, buf_ref, wmix_ref, scale_ref, o_ref, xs_ref, *, tl, start_pos, group, has_halo):
    i = pl.program_id(1)
    bn = u_ref.shape[0]

    @pl.when(i == 0)
    def _():
        xs_ref[:, 0:POOL_HALO, :] = buf_ref[...]

    if has_halo:
        @pl.when(i > 0)
        def _():
            xs_ref[:, 0:POOL_HALO, :] = halo_ref[...]

    u = u_ref[...]
    xs_ref[:, POOL_HALO:, :] = u
    pos = start_pos + i * tl + lax.broadcasted_iota(jnp.int32, (1, tl, 1), 1)
    for g, w in enumerate(POOL_WINDOWS):
        c0, c1 = g * group, (g + 1) * group
        ug = u[:, :, c0:c1]
        s = ug
        for j in range(1, w):
            s = s + xs_ref[:, POOL_HALO - j:POOL_HALO - j + tl, c0:c1]
        cnt = jnp.minimum(pos + 1, w).astype(F32)
        dlt = (s / cnt - ug).reshape(bn * tl, group).astype(MXU_DTYPE)
        y = jnp.dot(dlt, wmix_ref[g].astype(MXU_DTYPE), preferred_element_type=F32)
        y = y * scale_ref[:, c0:c1]
        o_ref[:, :, c0:c1] = y.reshape(bn, tl, group).astype(o_ref.dtype)


def _pool(u_arr, buf16, wmix, scale, *, n, l, width, start_pos, bn, tl):
    group = width // len(POOL_WINDOWS)
    has_halo = l > tl
    assert not has_halo or tl % POOL_HALO == 0
    halo_blocks = tl // POOL_HALO
    return pl.pallas_call(
        functools.partial(_pool_kernel, tl=tl, start_pos=start_pos, group=group, has_halo=has_halo),
        out_shape=jax.ShapeDtypeStruct((n, l, width), MXU_DTYPE),
        grid=(n // bn, l // tl),
        in_specs=[
            pl.BlockSpec((bn, tl, width), lambda b, i: (b, i, 0)),
            pl.BlockSpec((bn, min(POOL_HALO, l), width), lambda b, i: (b, jnp.maximum(i * halo_blocks - 1, 0), 0)),
            pl.BlockSpec((bn, POOL_HALO, width), lambda b, i: (b, 0, 0)),
            pl.BlockSpec(wmix.shape, lambda b, i: (0, 0, 0)),
            pl.BlockSpec((1, width), lambda b, i: (0, 0)),
        ],
        out_specs=pl.BlockSpec((bn, tl, width), lambda b, i: (b, i, 0)),
        scratch_shapes=[pltpu.VMEM((bn, tl + POOL_HALO, width), F32)],
        compiler_params=_cp("parallel", "arbitrary"),
        name="pool_mixer",
    )(u_arr, u_arr, buf16, wmix, scale)


def _lam_kernel(q1_ref, k1_ref, q2_ref, k2_ref, o_ref, *, lam_init):
    a = jnp.sum(q1_ref[...] * k1_ref[...], -1, keepdims=True)
    b = jnp.sum(q2_ref[...] * k2_ref[...], -1, keepdims=True)
    lam = jnp.exp(a) - jnp.exp(b) + lam_init
    o_ref[...] = jnp.broadcast_to(lam, o_ref.shape)


def _lam(q1, k1, q2, k2, lam_init):
    return pl.pallas_call(
        functools.partial(_lam_kernel, lam_init=lam_init),
        out_shape=jax.ShapeDtypeStruct((1, LANES), F32),
        name="diff_lambda",
    )(q1, k1, q2, k2)


def _softmax_update(s, v, m_ref, l_ref, acc_ref, idx):
    m_old = m_ref[idx]
    m_new = jnp.maximum(m_old, jnp.max(s, -1, keepdims=True))
    alpha = jnp.exp(m_old - m_new)
    p = jnp.exp(s - m_new)
    l_ref[idx] = alpha * l_ref[idx] + jnp.sum(p, -1, keepdims=True)
    acc_ref[idx] = alpha * acc_ref[idx] + jnp.dot(p.astype(MXU_DTYPE), v, preferred_element_type=F32)
    m_ref[idx] = m_new


def _diff_finalize(o1, o2, lam, gain, lam_init):
    o = o1 - lam * o2
    o = o * lax.rsqrt(jnp.mean(o * o, -1, keepdims=True) + RMS_EPS)
    return o * gain * (1.0 - lam_init)


def _attn_prompt_kernel(q_ref, k_ref, v_ref, bias_ref, bfar_ref, lam_ref, gain_ref, o_ref,
                        kb_ref, vb_ref, m_ref, l_ref, acc_ref, *, tq, nd_near, scale, lam_init):
    qi = pl.program_id(2)

    @pl.when(qi == 0)
    def _():
        kb_ref[...] = k_ref[0, 0].astype(MXU_DTYPE)
        vb_ref[...] = v_ref[0, 0].astype(MXU_DTYPE)

    qs = (q_ref[0] * scale).astype(MXU_DTYPE)
    qh = (qs[:, :HEAD_DIM], qs[:, HEAD_DIM:])
    m_ref[...] = jnp.full(m_ref.shape, -jnp.inf, F32)
    l_ref[...] = jnp.zeros_like(l_ref)
    acc_ref[...] = jnp.zeros_like(acc_ref)

    def key_tile(kj, bias):
        off = pl.multiple_of(kj * tq, tq)
        kt = kb_ref[pl.ds(off, tq), :]
        vt = vb_ref[pl.ds(off, tq), :]
        for half in range(2):
            kh = kt[:, half * HEAD_DIM:(half + 1) * HEAD_DIM]
            s = lax.dot_general(qh[half], kh, (((1,), (1,)), ((), ())), preferred_element_type=F32) + bias
            _softmax_update(s, vt, m_ref, l_ref, acc_ref, half)

    def far_body(kj, carry):
        key_tile(kj, bfar_ref[0][:, 0:1])
        return carry

    def near_body(kj, carry):
        key_tile(kj, bias_ref[0, qi - kj])
        return carry

    n_far = jnp.maximum(qi - (nd_near - 1), 0)
    lax.fori_loop(0, n_far, far_body, 0)
    lax.fori_loop(n_far, qi + 1, near_body, 0)
    o1 = acc_ref[0] / l_ref[0]
    o2 = acc_ref[1] / l_ref[1]
    o_ref[0] = _diff_finalize(o1, o2, lam_ref[:, 0:1], gain_ref[...], lam_init).astype(o_ref.dtype)


def _attn_prompt(q_arr, q_col0, k_arr, v_arr, bias_near, bias_far, lam, gain, *, b, l, n_heads, tq, lam_init):
    qcb = q_col0 // QK_HEAD
    nd = l // tq
    nd_near = bias_near.shape[1]
    return pl.pallas_call(
        functools.partial(_attn_prompt_kernel, tq=tq, nd_near=nd_near, scale=HEAD_DIM ** -0.5, lam_init=lam_init),
        out_shape=jax.ShapeDtypeStruct((b, l, n_heads * V_HEAD), MXU_DTYPE),
        grid=(b, n_heads, nd),
        in_specs=[
            pl.BlockSpec((1, tq, QK_HEAD), lambda bi, h, qi: (bi, qi, qcb + h)),
            pl.BlockSpec((1, 1, l, QK_HEAD), lambda bi, h, qi: (bi, h, 0, 0)),
            pl.BlockSpec((1, 1, l, V_HEAD), lambda bi, h, qi: (bi, h, 0, 0)),
            pl.BlockSpec((1, nd_near, tq, tq), lambda bi, h, qi: (h, 0, 0, 0)),
            pl.BlockSpec((1, 1, LANES), lambda bi, h, qi: (h, 0, 0)),
            pl.BlockSpec((1, LANES), lambda bi, h, qi: (0, 0)),
            pl.BlockSpec((1, V_HEAD), lambda bi, h, qi: (0, 0)),
        ],
        out_specs=pl.BlockSpec((1, tq, V_HEAD), lambda bi, h, qi: (bi, qi, h)),
        scratch_shapes=[
            pltpu.VMEM((l, QK_HEAD), MXU_DTYPE),
            pltpu.VMEM((l, V_HEAD), MXU_DTYPE),
            pltpu.VMEM((2, tq, 1), F32),
            pltpu.VMEM((2, tq, 1), F32),
            pltpu.VMEM((2, tq, V_HEAD), F32),
        ],
        compiler_params=_cp("parallel", "parallel", "arbitrary"),
        name="attn_prompt",
    )(q_arr, k_arr, v_arr, bias_near, bias_far, lam, gain)


def _attn_sample_kernel(pt_ref, q_ref, kn_ref, vn_ref, kc_ref, vc_ref, bias_ref, biasn_ref, lam_ref, gain_ref,
                        o_ref, qb_ref, kpad_ref, vpad_ref, m_ref, l_ref, acc_ref,
                        *, n_heads, n_pages, ls, scale, lam_init):
    del pt_ref
    si = pl.program_id(0)
    j = pl.program_id(1)

    @pl.when(jnp.logical_and(si == 0, j == 0))
    def _():
        kpad_ref[...] = jnp.zeros_like(kpad_ref)
        vpad_ref[...] = jnp.zeros_like(vpad_ref)

    @pl.when(j == 0)
    def _():
        qb_ref[...] = q_ref[0] * scale
        m_ref[...] = jnp.full(m_ref.shape, -jnp.inf, F32)
        l_ref[...] = jnp.zeros_like(l_ref)
        acc_ref[...] = jnp.zeros_like(acc_ref)
        kpad_ref[:, 0:ls, :] = kn_ref[0]
        vpad_ref[:, 0:ls, :] = vn_ref[0]

    def page(k_get, v_get, b_get):
        for h in range(n_heads):
            c0 = h * QK_HEAD
            kh = k_get(h).astype(MXU_DTYPE)
            vh = v_get(h).astype(MXU_DTYPE)
            qh = qb_ref[:, c0:c0 + QK_HEAD].astype(MXU_DTYPE)
            s1 = lax.dot_general(qh[:, :HEAD_DIM], kh[:, :HEAD_DIM], (((1,), (1,)), ((), ())),
                                 preferred_element_type=F32)
            s2 = lax.dot_general(qh[:, HEAD_DIM:], kh[:, HEAD_DIM:], (((1,), (1,)), ((), ())),
                                 preferred_element_type=F32)
            s = jnp.concatenate([s1, s2], axis=0) + b_get(h)
            _softmax_update(s, vh, m_ref, l_ref, acc_ref, h)

    @pl.when(j < n_pages)
    def _():
        page(lambda h: kc_ref[0, h], lambda h: vc_ref[0, h], lambda h: bias_ref[0, h])

    @pl.when(j == n_pages)
    def _():
        page(lambda h: kpad_ref[h], lambda h: vpad_ref[h], lambda h: biasn_ref[h])
        lam = lam_ref[:, 0:1]
        for h in range(n_heads):
            o = acc_ref[h] / l_ref[h]
            y = _diff_finalize(o[0:ls], o[ls:2 * ls], lam, gain_ref[...], lam_init)
            o_ref[0, :, h * V_HEAD:(h + 1) * V_HEAD] = y.astype(o_ref.dtype)


def _attn_sample(page_table, q, kn, vn, cache_k, cache_v, bias_pages, bias_new, lam, gain, *, lam_init):
    s, ls, width = q.shape
    n_heads = width // QK_HEAD
    n_pages = page_table.shape[1]
    page = cache_k.shape[2]
    last = n_pages - 1
    grid_spec = pltpu.PrefetchScalarGridSpec(
        num_scalar_prefetch=1,
        grid=(s, n_pages + 1),
        in_specs=[
            pl.BlockSpec((1, ls, width), lambda si, j, pt: (si, 0, 0)),
            pl.BlockSpec((1, n_heads, ls, QK_HEAD), lambda si, j, pt: (si, 0, 0, 0)),
            pl.BlockSpec((1, n_heads, ls, V_HEAD), lambda si, j, pt: (si, 0, 0, 0)),
            pl.BlockSpec((1, n_heads, page, QK_HEAD),
                         lambda si, j, pt: (pt[si * n_pages + jnp.minimum(j, last)], 0, 0, 0)),
            pl.BlockSpec((1, n_heads, page, V_HEAD),
                         lambda si, j, pt: (pt[si * n_pages + jnp.minimum(j, last)], 0, 0, 0)),
            pl.BlockSpec((1, n_heads, 2 * ls, page), lambda si, j, pt: (jnp.minimum(j, last), 0, 0, 0)),
            pl.BlockSpec((n_heads, 2 * ls, page), lambda si, j, pt: (0, 0, 0)),
            pl.BlockSpec((1, LANES), lambda si, j, pt: (0, 0)),
            pl.BlockSpec((1, V_HEAD), lambda si, j, pt: (0, 0)),
        ],
        out_specs=pl.BlockSpec((1, ls, width), lambda si, j, pt: (si, 0, 0)),
        scratch_shapes=[
            pltpu.VMEM((ls, width), F32),
            pltpu.VMEM((n_heads, page, QK_HEAD), F32),
            pltpu.VMEM((n_heads, page, V_HEAD), F32),
            pltpu.VMEM((n_heads, 2 * ls, 1), F32),
            pltpu.VMEM((n_heads, 2 * ls, 1), F32),
            pltpu.VMEM((n_heads, 2 * ls, V_HEAD), F32),
        ],
    )
    return pl.pallas_call(
        functools.partial(_attn_sample_kernel, n_heads=n_heads, n_pages=n_pages, ls=ls,
                          scale=HEAD_DIM ** -0.5, lam_init=lam_init),
        out_shape=jax.ShapeDtypeStruct((s, ls, width), MXU_DTYPE),
        grid_spec=grid_spec,
        compiler_params=_cp("arbitrary", "arbitrary"),
        name="attn_sample",
    )(page_table.reshape(-1), q, kn, vn, cache_k, cache_v, bias_pages, bias_new, lam, gain)


def _mm2_kernel(a1_ref, a2_ref, w_ref, o_ref, acc_ref, *, n1, nk):
    k = pl.program_id(2)

    @pl.when(k == 0)
    def _():
        acc_ref[...] = jnp.zeros_like(acc_ref)

    wb = w_ref[...].astype(MXU_DTYPE)

    @pl.when(k < n1)
    def _():
        acc_ref[...] += jnp.dot(a1_ref[...], wb, preferred_element_type=F32)

    @pl.when(k >= n1)
    def _():
        acc_ref[...] += jnp.dot(a2_ref[...], wb, preferred_element_type=F32)

    @pl.when(k == nk - 1)
    def _():
        o_ref[...] = acc_ref[...]


def _mm2(a1, a2, w, *, tm, tn, tk):
    rows, k1 = a1.shape
    k2 = a2.shape[1]
    cols = w.shape[1]
    tm, tn = _tile(rows, tm), _tile(cols, tn, LANES)
    tk = _tile(math.gcd(k1, k2), tk, LANES)
    n1, n2 = k1 // tk, k2 // tk
    return pl.pallas_call(
        functools.partial(_mm2_kernel, n1=n1, nk=n1 + n2),
        out_shape=jax.ShapeDtypeStruct((rows, cols), F32),
        grid=(rows // tm, cols // tn, n1 + n2),
        in_specs=[
            pl.BlockSpec((tm, tk), lambda i, j, k: (i, jnp.minimum(k, n1 - 1))),
            pl.BlockSpec((tm, tk), lambda i, j, k: (i, jnp.maximum(k - n1, 0))),
            pl.BlockSpec((tk, tn), lambda i, j, k: (k, j)),
        ],
        out_specs=pl.BlockSpec((tm, tn), lambda i, j, k: (i, j)),
        scratch_shapes=[pltpu.VMEM((tm, tn), F32)],
        compiler_params=_cp("parallel", "parallel", "arbitrary"),
        name="mm_out_proj",
    )(a1, a2, w)


def _ln1_kernel(h_ref, mp_ref, ms_ref, g_ref, b_ref, o_ref, ob_ref, *, n_p, alpha):
    i = pl.program_id(0)

    def run(m_ref):
        y = _ln_rows(alpha * h_ref[...] + m_ref[...], g_ref[...], b_ref[...])
        o_ref[...] = y
        ob_ref[...] = y.astype(ob_ref.dtype)

    @pl.when(i < n_p)
    def _():
        run(mp_ref)

    @pl.when(i >= n_p)
    def _():
        run(ms_ref)


def _ln1(h, mix_p, mix_s, g, b, alpha, tm):
    t, d = h.shape
    n_p, n_s = mix_p.shape[0] // tm, mix_s.shape[0] // tm
    return pl.pallas_call(
        functools.partial(_ln1_kernel, n_p=n_p, alpha=alpha),
        out_shape=(jax.ShapeDtypeStruct((t, d), F32), jax.ShapeDtypeStruct((t, d), MXU_DTYPE)),
        grid=(n_p + n_s,),
        in_specs=[
            pl.BlockSpec((tm, d), lambda i: (i, 0)),
            pl.BlockSpec((tm, d), lambda i: (jnp.minimum(i, n_p - 1), 0)),
            pl.BlockSpec((tm, d), lambda i: (jnp.maximum(i - n_p, 0), 0)),
            pl.BlockSpec((1, d), lambda i: (0, 0)),
            pl.BlockSpec((1, d), lambda i: (0, 0)),
        ],
        out_specs=(pl.BlockSpec((tm, d), lambda i: (i, 0)), pl.BlockSpec((tm, d), lambda i: (i, 0))),
        compiler_params=_cp("arbitrary"),
        name="ln1",
    )(h, mix_p, mix_s, g, b)


def _router_kernel(h_ref, w_ref, b_ref, ri_ref, rw_ref, cnt_ref, carry_ref, *, tm):
    i = pl.program_id(0)

    @pl.when(i == 0)
    def _():
        carry_ref[...] = jnp.zeros_like(carry_ref)

    logits = jnp.dot(h_ref[...], w_ref[...], preferred_element_type=F32,
                     precision=lax.Precision.HIGHEST) + b_ref[...]
    col = lax.broadcasted_iota(jnp.int32, logits.shape, 1).astype(F32)

    def first_argmax(x):
        mx = jnp.max(x, -1, keepdims=True)
        return mx, jnp.min(jnp.where(x == mx, col, float(LANES)), -1, keepdims=True)

    lg = jnp.where(col < N_GROUPS, logits, -jnp.inf)
    gmax, gidx = first_argmax(lg)
    g_w = 1.0 / jnp.sum(jnp.exp(lg - gmax), -1, keepdims=True)
    e_lo = N_GROUPS + gidx * EXPERTS_PER_GROUP
    le = jnp.where(jnp.logical_and(col >= e_lo, col < e_lo + EXPERTS_PER_GROUP), logits, -jnp.inf)
    v1, i1 = first_argmax(le)
    v2, i2 = first_argmax(jnp.where(col == i1, -jnp.inf, le))
    e21 = jnp.exp(v2 - v1)
    w1 = g_w / (1.0 + e21)
    w2 = g_w * e21 / (1.0 + e21)
    e1 = i1 - N_GROUPS
    e2 = i2 - N_GROUPS

    onehot = jnp.where(jnp.logical_or(col == e1, col == e2), 1.0, 0.0)
    r = lax.broadcasted_iota(jnp.int32, (tm, tm), 0)
    c = lax.broadcasted_iota(jnp.int32, (tm, tm), 1)
    tril = jnp.where(c < r, 1.0, 0.0).astype(MXU_DTYPE)
    before = jnp.dot(tril, onehot.astype(MXU_DTYPE), preferred_element_type=F32) + carry_ref[...]
    rank1 = jnp.sum(jnp.where(col == e1, before, 0.0), -1, keepdims=True)
    rank2 = jnp.sum(jnp.where(col == e2, before, 0.0), -1, keepdims=True)
    carry_ref[...] += jnp.sum(onehot, 0, keepdims=True)

    ri = jnp.where(col == 0, e1, jnp.where(col == 1, e2, jnp.where(col == 2, rank1, jnp.where(col == 3, rank2, 0.0))))
    ri_ref[...] = ri.astype(jnp.int32)
    rw_ref[...] = jnp.where(col == 0, w1, jnp.where(col == 1, w2, 0.0))
    cnt_ref[...] = carry_ref[...]


def _router(h1, w_r, b_r, tm):
    t, d = h1.shape
    return pl.pallas_call(
        functools.partial(_router_kernel, tm=tm),
        out_shape=(jax.ShapeDtypeStruct((t, LANES), jnp.int32), jax.ShapeDtypeStruct((t, LANES), F32),
                   jax.ShapeDtypeStruct((1, LANES), F32)),
        grid=(t // tm,),
        in_specs=[
            pl.BlockSpec((tm, d), lambda i: (i, 0)),
            pl.BlockSpec((d, LANES), lambda i: (0, 0)),
            pl.BlockSpec((1, LANES), lambda i: (0, 0)),
        ],
        out_specs=(pl.BlockSpec((tm, LANES), lambda i: (i, 0)), pl.BlockSpec((tm, LANES), lambda i: (i, 0)),
                   pl.BlockSpec((1, LANES), lambda i: (0, 0))),
        scratch_shapes=[pltpu.VMEM((1, LANES), F32)],
        compiler_params=_cp("arbitrary"),
        name="router",
    )(h1, w_r, b_r)


def _row_copy(src_hbm, row, buf, r, sem):
    return pltpu.make_async_copy(src_hbm.at[pl.ds(row, 1), :], buf.at[pl.ds(r, 1), :], sem)


def _gather_kernel(idx_ref, src_hbm, o_ref, buf_ref, sem, *, tm):
    base = pl.program_id(0) * tm

    def issue(r, carry):
        _row_copy(src_hbm, idx_ref[base + r], buf_ref, r, sem).start()
        return carry

    def wait(r, carry):
        _row_copy(src_hbm, 0, buf_ref, r, sem).wait()
        return carry

    lax.fori_loop(0, tm, issue, 0)
    lax.fori_loop(0, tm, wait, 0)
    o_ref[...] = buf_ref[...].astype(o_ref.dtype)


def _gather_rows(idx, src, tm):
    p = idx.shape[0]
    d = src.shape[1]
    grid_spec = pltpu.PrefetchScalarGridSpec(
        num_scalar_prefetch=1,
        grid=(p // tm,),
        in_specs=[pl.BlockSpec(memory_space=pl.ANY)],
        out_specs=pl.BlockSpec((tm, d), lambda i, idx: (i, 0)),
        scratch_shapes=[pltpu.VMEM((tm, d), src.dtype), pltpu.SemaphoreType.DMA],
    )
    return pl.pallas_call(
        functools.partial(_gather_kernel, tm=tm),
        out_shape=jax.ShapeDtypeStruct((p, d), MXU_DTYPE),
        grid_spec=grid_spec,
        compiler_params=_cp("arbitrary"),
        name="moe_gather",
    )(idx, src)


def _expert_changed(te_ref, i):
    prev = te_ref[jnp.maximum(i - 1, 0)]
    return jnp.logical_or(i == 0, te_ref[i] != prev)


def _gm1_kernel(te_ref, nu_ref, x_ref, wg_ref, wu_ref, rw_ref, o_ref, wgb_ref, wub_ref):
    i = pl.program_id(1)
    used = i < nu_ref[0]

    @pl.when(jnp.logical_and(used, _expert_changed(te_ref, i)))
    def _():
        wgb_ref[...] = wg_ref[0].astype(MXU_DTYPE)
        wub_ref[...] = wu_ref[0].astype(MXU_DTYPE)

    @pl.when(used)
    def _():
        x = x_ref[...]
        hg = jnp.dot(x, wgb_ref[...], preferred_element_type=F32)
        hu = jnp.dot(x, wub_ref[...], preferred_element_type=F32)
        a = hg * jax.nn.sigmoid(hg) * hu * rw_ref[...]
        o_ref[...] = a.astype(o_ref.dtype)

    @pl.when(jnp.logical_not(used))
    def _():
        o_ref[...] = jnp.zeros_like(o_ref)


def _gm1(te, nu, x, wg, wu, rw, *, tm, tn):
    p, d = x.shape
    ff = wg.shape[2]
    tn = _tile(ff, tn, LANES)
    nt = p // tm

    def row(i, nu):
        return jnp.minimum(i, nu[0] - 1)

    grid_spec = pltpu.PrefetchScalarGridSpec(
        num_scalar_prefetch=2,
        grid=(ff // tn, nt),
        in_specs=[
            pl.BlockSpec((tm, d), lambda j, i, te, nu: (row(i, nu), 0)),
            pl.BlockSpec((1, d, tn), lambda j, i, te, nu: (te[row(i, nu)], 0, j)),
            pl.BlockSpec((1, d, tn), lambda j, i, te, nu: (te[row(i, nu)], 0, j)),
            pl.BlockSpec((tm, 1), lambda j, i, te, nu: (row(i, nu), 0)),
        ],
        out_specs=pl.BlockSpec((tm, tn), lambda j, i, te, nu: (i, j)),
        scratch_shapes=[pltpu.VMEM((d, tn), MXU_DTYPE), pltpu.VMEM((d, tn), MXU_DTYPE)],
    )
    return pl.pallas_call(
        _gm1_kernel,
        out_shape=jax.ShapeDtypeStruct((p, ff), MXU_DTYPE),
        grid_spec=grid_spec,
        compiler_params=_cp("arbitrary", "arbitrary"),
        name="moe_gate_up",
    )(te, nu, x, wg, wu, rw)


def _gm2_kernel(te_ref, nu_ref, a_ref, wd_ref, o_ref, wdb_ref):
    i = pl.program_id(1)
    used = i < nu_ref[0]

    @pl.when(jnp.logical_and(used, _expert_changed(te_ref, i)))
    def _():
        wdb_ref[...] = wd_ref[0].astype(MXU_DTYPE)

    @pl.when(used)
    def _():
        o_ref[...] = jnp.dot(a_ref[...], wdb_ref[...], preferred_element_type=F32)

    @pl.when(jnp.logical_not(used))
    def _():
        o_ref[...] = jnp.zeros_like(o_ref)


def _gm2(te, nu, a, wd, *, tm, tn):
    p, ff = a.shape
    d = wd.shape[2]
    tn = _tile(d, tn, LANES)
    nt = p // tm

    def row(i, nu):
        return jnp.minimum(i, nu[0] - 1)

    grid_spec = pltpu.PrefetchScalarGridSpec(
        num_scalar_prefetch=2,
        grid=(d // tn, nt),
        in_specs=[
            pl.BlockSpec((tm, ff), lambda j, i, te, nu: (row(i, nu), 0)),
            pl.BlockSpec((1, ff, tn), lambda j, i, te, nu: (te[row(i, nu)], 0, j)),
        ],
        out_specs=pl.BlockSpec((tm, tn), lambda j, i, te, nu: (i, j)),
        scratch_shapes=[pltpu.VMEM((ff, tn), MXU_DTYPE)],
    )
    return pl.pallas_call(
        _gm2_kernel,
        out_shape=jax.ShapeDtypeStruct((p, d), F32),
        grid_spec=grid_spec,
        compiler_params=_cp("arbitrary", "arbitrary"),
        name="moe_down",
    )(te, nu, a, wd)


def _combine_ln2_kernel(pos_ref, h_ref, y_hbm, g_ref, b_ref, o_ref, ob_ref, buf_ref, sem, *, tm, alpha):
    base = pl.program_id(0) * tm

    def issue(r, carry):
        for s in range(2):
            _row_copy(y_hbm, pos_ref[2 * (base + r) + s], buf_ref.at[s], r, sem).start()
        return carry

    def wait(r, carry):
        for s in range(2):
            _row_copy(y_hbm, 0, buf_ref.at[s], r, sem).wait()
        return carry

    lax.fori_loop(0, tm, issue, 0)
    lax.fori_loop(0, tm, wait, 0)
    y = _ln_rows(alpha * h_ref[...] + (buf_ref[0] + buf_ref[1]), g_ref[...], b_ref[...])
    o_ref[...] = y
    ob_ref[...] = y.astype(ob_ref.dtype)


def _combine_ln2(pos, h1, y_sorted, g, b, alpha, tm):
    t, d = h1.shape
    grid_spec = pltpu.PrefetchScalarGridSpec(
        num_scalar_prefetch=1,
        grid=(t // tm,),
        in_specs=[
            pl.BlockSpec((tm, d), lambda i, pos: (i, 0)),
            pl.BlockSpec(memory_space=pl.ANY),
            pl.BlockSpec((1, d), lambda i, pos: (0, 0)),
            pl.BlockSpec((1, d), lambda i, pos: (0, 0)),
        ],
        out_specs=(pl.BlockSpec((tm, d), lambda i, pos: (i, 0)), pl.BlockSpec((tm, d), lambda i, pos: (i, 0))),
        scratch_shapes=[pltpu.VMEM((2, tm, d), F32), pltpu.SemaphoreType.DMA],
    )
    return pl.pallas_call(
        functools.partial(_combine_ln2_kernel, tm=tm, alpha=alpha),
        out_shape=(jax.ShapeDtypeStruct((t, d), F32), jax.ShapeDtypeStruct((t, d), MXU_DTYPE)),
        grid_spec=grid_spec,
        compiler_params=_cp("arbitrary"),
        name="moe_combine_ln2",
    )(pos, h1, y_sorted, g, b)


def _ple_kernel(hb_ref, wg_ref, p_ref, wp_ref, bg_ref, h_ref, o_ref, acc_ref, *, nk):
    k = pl.program_id(2)

    @pl.when(k == 0)
    def _():
        acc_ref[...] = jnp.zeros_like(acc_ref)

    acc_ref[...] += jnp.dot(hb_ref[...], wg_ref[...].astype(MXU_DTYPE), preferred_element_type=F32)

    @pl.when(k == nk - 1)
    def _():
        proj = jnp.dot(p_ref[...].astype(MXU_DTYPE), wp_ref[...].astype(MXU_DTYPE), preferred_element_type=F32)
        gate = jax.nn.sigmoid(acc_ref[...] + bg_ref[...])
        o_ref[...] = h_ref[...] + gate * proj


def _ple(h2, h2b, p, wg, wp, bg, *, row0, rows, tm, tn, tk):
    d = h2.shape[1]
    pd = p.shape[1]
    tm, tn, tk = _tile(rows, tm), _tile(d, tn, LANES), _tile(d, tk, LANES)
    assert row0 % tm == 0
    rb, nk = row0 // tm, d // tk
    return pl.pallas_call(
        functools.partial(_ple_kernel, nk=nk),
        out_shape=jax.ShapeDtypeStruct((rows, d), F32),
        grid=(rows // tm, d // tn, nk),
        in_specs=[
            pl.BlockSpec((tm, tk), lambda i, j, k: (i + rb, k)),
            pl.BlockSpec((tk, tn), lambda i, j, k: (k, j)),
            pl.BlockSpec((tm, pd), lambda i, j, k: (i, 0)),
            pl.BlockSpec((pd, tn), lambda i, j, k: (0, j)),
            pl.BlockSpec((1, tn), lambda i, j, k: (0, j)),
            pl.BlockSpec((tm, tn), lambda i, j, k: (i + rb, j)),
        ],
        out_specs=pl.BlockSpec((tm, tn), lambda i, j, k: (i, j)),
        scratch_shapes=[pltpu.VMEM((tm, tn), F32)],
        compiler_params=_cp("parallel", "parallel", "arbitrary"),
        name="ple",
    )(h2b, wg, p, wp, bg, h2)


def _bias_by_distance(q_pos, k_pos, table):
    dist = q_pos[:, None] - k_pos[None, :]
    n = jnp.maximum(dist, 0)
    max_exact = NUM_BUCKETS // 2
    nf = jnp.maximum(n, 1).astype(F32)
    large = max_exact + (jnp.log(nf / max_exact) / math.log(MAX_DISTANCE / max_exact)
                         * (NUM_BUCKETS - max_exact)).astype(jnp.int32)
    large = jnp.minimum(large, NUM_BUCKETS - 1)
    bucket = jnp.where(n < max_exact, n, large)
    onehot = jax.nn.one_hot(bucket, NUM_BUCKETS, dtype=F32)
    bias = jnp.einsum("qkb,bh->hqk", onehot, table.astype(F32), precision=lax.Precision.HIGHEST)
    return jnp.where((dist >= 0)[None], bias, NEG)


def kernel(x_prompt, x_sample, p_prompt, p_sample, cache_k, cache_v, state_pool, page_table, ln_in_g, ln_in_b,
           w_in, w_pool_mix, pool_scale, lambda_q1, lambda_k1, lambda_q2, lambda_k2, head_gain, rel_bias, w_out,
           ln1_g, ln1_b, router_group_w, router_group_b, router_expert_w, router_expert_b, w_gate, w_up, w_down,
           ln2_g, ln2_b, w_ple_proj, w_ple_gate, b_ple_gate):
    bsz, seq, d = x_prompt.shape
    dec_b, dec_l, _ = x_sample.shape
    depth = w_in.shape[0]
    n_pool, page_size, n_heads, _ = cache_k.shape[1:]
    n_pages = page_table.shape[1]
    past_len = n_pages * page_size
    pool_buf, pool_w = state_pool.shape[2:]
    attn_w = n_heads * V_HEAD
    qk_w = n_heads * QK_HEAD
    n_experts = w_gate.shape[1]
    tp, ts = bsz * seq, dec_b * dec_l
    t = tp + ts
    alpha = (2.0 * depth) ** 0.25
    assert pool_buf == max(POOL_WINDOWS) - 1 and pool_w + attn_w == d
    assert n_experts == N_GROUPS * EXPERTS_PER_GROUP

    row_tile = _tile(math.gcd(tp, ts), 256)
    tq = _tile(seq, 256)
    moe_tm = 256
    n_tiles = (2 * t) // moe_tm + n_experts
    p_rows = n_tiles * moe_tm

    as_row = lambda v: v.reshape(1, -1)
    h, hb = _ln_in(x_prompt.reshape(tp, d), x_sample.reshape(ts, d), as_row(ln_in_g), as_row(ln_in_b), row_tile)

    kp_l, vp_l, ks_l, vs_l, bp_l, bs_l = [], [], [], [], [], []
    for l in range(depth):
        lam_init = 0.8 - 0.6 * math.exp(-0.3 * l)
        lam = _lam(as_row(lambda_q1[l]), as_row(lambda_k1[l]), as_row(lambda_q2[l]), as_row(lambda_k2[l]), lam_init)
        gain = as_row(head_gain[l])
        w_in_l = w_in[l]
        c_q, c_k, c_v = pool_w, pool_w + qk_w, pool_w + 2 * qk_w
        mm_in = functools.partial(_mm, hb, w_in_l, tm=1024, tn=1024, tk=512)
        mm_in_heads = functools.partial(_mm_heads, hb, w_in_l, n_heads=n_heads, tm=1024, tn=1024, tk=512)

        uq_p = mm_in(row0=0, rows=tp, col0=0, cols=c_k, name="mm_in_uq_p")
        k_p = mm_in_heads(row0=0, nb=bsz, l=seq, col0=c_k, head_w=QK_HEAD, name="mm_in_k_p")
        v_p = mm_in_heads(row0=0, nb=bsz, l=seq, col0=c_v, head_w=V_HEAD, name="mm_in_v_p")
        uq_p3 = uq_p.reshape(bsz, seq, c_k)
        buf16_p = jnp.zeros((bsz, POOL_HALO, pool_w), F32)
        pool_p = _pool(uq_p3, buf16_p, w_pool_mix[l], as_row(pool_scale[l]), n=bsz, l=seq, width=pool_w,
                       start_pos=0, bn=1, tl=_tile(seq, 256, POOL_HALO))
        pos_q = jnp.arange(tq)
        nd_near = min(seq // tq, 1 + -(-(MAX_DISTANCE - 1) // tq))
        bias_near = jnp.stack([_bias_by_distance(dd * tq + pos_q, pos_q, rel_bias) for dd in range(nd_near)], axis=1)
        bias_far = jnp.broadcast_to(rel_bias[NUM_BUCKETS - 1].astype(F32)[:, None, None], (n_heads, 1, LANES))
        of_p = _attn_prompt(uq_p3, c_q, k_p, v_p, bias_near, bias_far, lam, gain, b=bsz, l=seq, n_heads=n_heads,
                            tq=tq, lam_init=lam_init)

        u_s = mm_in(row0=tp, rows=ts, col0=0, cols=pool_w, name="mm_in_u_s")
        q_s = mm_in(row0=tp, rows=ts, col0=c_q, cols=qk_w, name="mm_in_q_s")
        k_s = mm_in_heads(row0=tp, nb=dec_b, l=dec_l, col0=c_k, head_w=QK_HEAD, name="mm_in_k_s")
        v_s = mm_in_heads(row0=tp, nb=dec_b, l=dec_l, col0=c_v, head_w=V_HEAD, name="mm_in_v_s")
        u_s3 = u_s.reshape(dec_b, dec_l, pool_w)
        buf16_s = jnp.concatenate([jnp.zeros((dec_b, POOL_HALO - pool_buf, pool_w), F32), state_pool[l]], axis=1)
        pool_s = _pool(u_s3, buf16_s, w_pool_mix[l], as_row(pool_scale[l]), n=dec_b, l=dec_l, width=pool_w,
                       start_pos=past_len, bn=_tile(dec_b, 16, 1), tl=dec_l)
        q_pos = past_len + jnp.arange(dec_l)
        bias_c = _bias_by_distance(q_pos, jnp.arange(past_len), rel_bias)
        bias_c = bias_c.reshape(n_heads, dec_l, n_pages, page_size).transpose(2, 0, 1, 3)
        bias_c = jnp.concatenate([bias_c, bias_c], axis=2)
        k_new_pos = jnp.where(jnp.arange(page_size) < dec_l, past_len + jnp.arange(page_size), past_len + 2 * page_size)
        bias_n = _bias_by_distance(q_pos, k_new_pos, rel_bias)
        bias_n = jnp.concatenate([bias_n, bias_n], axis=1)
        of_s = _attn_sample(page_table, q_s.reshape(dec_b, dec_l, qk_w), k_s, v_s,
                            jnp.transpose(cache_k[l], (0, 2, 1, 3)), jnp.transpose(cache_v[l], (0, 2, 1, 3)),
                            bias_c, bias_n, lam, gain, lam_init=lam_init)

        mix_p = _mm2(pool_p.reshape(tp, pool_w), of_p.reshape(tp, attn_w), w_out[l], tm=1024, tn=1024, tk=512)
        mix_s = _mm2(pool_s.reshape(ts, pool_w), of_s.reshape(ts, attn_w), w_out[l], tm=1024, tn=1024, tk=512)
        h1, h1b = _ln1(h, mix_p, mix_s, as_row(ln1_g[l]), as_row(ln1_b[l]), alpha, row_tile)

        w_r = jnp.concatenate([router_group_w[l], jnp.moveaxis(router_expert_w[l], 0, 1).reshape(d, n_experts)], axis=1)
        b_r = jnp.concatenate([router_group_b[l], router_expert_b[l].reshape(-1)])
        n_r = w_r.shape[1]
        w_r = jnp.pad(w_r, ((0, 0), (0, LANES - n_r)))
        b_r = jnp.pad(b_r, (0, LANES - n_r)).reshape(1, LANES)
        r_int, r_w, counts = _router(h1, w_r, b_r, row_tile)
        counts = counts[0, :n_experts].astype(jnp.int32)
        tiles_per = (counts + moe_tm - 1) // moe_tm
        tile_end = jnp.cumsum(tiles_per)
        offsets = (tile_end - tiles_per) * moe_tm
        eid, rank = r_int[:, 0:2], r_int[:, 2:4]
        pos = offsets[eid] + rank
        nu = tile_end[-1:].astype(jnp.int32)
        te = jnp.sum(tile_end[None, :] <= jnp.arange(n_tiles)[:, None], axis=1)
        te = jnp.minimum(te, n_experts - 1).astype(jnp.int32)
        tok = jnp.broadcast_to(jnp.arange(t, dtype=jnp.int32)[:, None], (t, 2))
        row_src = jnp.zeros((p_rows,), jnp.int32).at[pos.reshape(-1)].set(tok.reshape(-1))
        row_w = jnp.zeros((p_rows,), F32).at[pos.reshape(-1)].set(r_w[:, 0:2].reshape(-1)).reshape(p_rows, 1)
        x_sorted = _gather_rows(row_src, h1, moe_tm)
        a_sorted = _gm1(te, nu, x_sorted, w_gate[l], w_up[l], row_w, tm=moe_tm, tn=256)
        y_sorted = _gm2(te, nu, a_sorted, w_down[l], tm=moe_tm, tn=1024)
        h2, h2b = _combine_ln2(pos.reshape(-1).astype(jnp.int32), h1, y_sorted, as_row(ln2_g[l]), as_row(ln2_b[l]),
                               alpha, _tile(math.gcd(tp, ts), 128))

        ple = functools.partial(_ple, h2, h2b, wg=w_ple_gate[l], wp=w_ple_proj[l], bg=as_row(b_ple_gate[l]),
                                tm=1024, tn=1024, tk=512)
        y_p = ple(p_prompt[l].reshape(tp, -1), row0=0, rows=tp)
        y_s = ple(p_sample[l].reshape(ts, -1), row0=tp, rows=ts)

        if l + 1 < depth:
            h = jnp.concatenate([y_p, y_s], axis=0)
            hb = h.astype(MXU_DTYPE)

        kp_l.append(jnp.transpose(k_p, (0, 2, 1, 3)))
        vp_l.append(jnp.transpose(v_p, (0, 2, 1, 3)))
        ks_l.append(jnp.transpose(k_s, (0, 2, 1, 3)))
        vs_l.append(jnp.transpose(v_s, (0, 2, 1, 3)))
        up_p = jnp.concatenate([jnp.zeros((bsz, pool_buf, pool_w), F32), uq_p3[:, max(seq - pool_buf, 0):, :pool_w]], axis=1)
        bp_l.append(up_p[:, -pool_buf:])
        bs_l.append(jnp.concatenate([state_pool[l], u_s3], axis=1)[:, -pool_buf:])

    return (y_p.reshape(bsz, seq, d), y_s.reshape(dec_b, dec_l, d), jnp.stack(kp_l), jnp.stack(vp_l),
            jnp.stack(ks_l), jnp.stack(vs_l), jnp.stack(bp_l), jnp.stack(bs_l))
```
